```python
import math, functools
import jax, jax.numpy as jnp
from jax import lax
import numpy as np

D_MODEL = 1024
BATCH = 8
SEQ = 4096
DEPTH = 1
DEC_BATCH = 128
DEC_SEQ = 4
PAST_LEN = 8192
PAGE_SIZE = 128

N_META = 16
ATTN_HEADS = 8
HEAD_DIM = 64
ATTN_WIDTH = ATTN_HEADS * HEAD_DIM
SSM_WIDTH = D_MODEL // 2
SSM_GROUP = 16
SSM_GROUPS = SSM_WIDTH // SSM_GROUP
SSM_STATE = 64
N_BRANCH = 2
D_FF = 4 * D_MODEL
Q_BLOCK = 128
RMS_EPS = 1e-6
MASK_VALUE = -1e30
DT_MIN = 1e-3
DT_MAX = 1e-1
IN_COLS = 3 * ATTN_WIDTH + ATTN_HEADS + SSM_WIDTH + N_BRANCH * D_MODEL

kernel_name = "fox_s5_meta_hybrid_step"


def _rmsnorm(x, gain):
    xf = x.astype(jnp.float32)
    inv = lax.rsqrt(jnp.mean(xf * xf, axis=-1, keepdims=True) + RMS_EPS)
    return (xf * inv * gain.astype(jnp.float32)).astype(x.dtype)


def _split_in(z):
    sizes = (ATTN_WIDTH, ATTN_WIDTH, ATTN_WIDTH, ATTN_HEADS, SSM_WIDTH, N_BRANCH * D_MODEL)
    points, acc = [], 0
    for s in sizes[:-1]:
        acc += s
        points.append(acc)
    return jnp.split(z, points, axis=-1)


def _fox_prompt(q, k, v, logf):
    B, T, H, Dh = q.shape
    pad = (-T) % Q_BLOCK
    Tp = T + pad
    nb = Tp // Q_BLOCK
    scale = HEAD_DIM ** -0.5
    pad4 = ((0, 0), (pad, 0), (0, 0), (0, 0))
    qf = jnp.pad(q.astype(jnp.float32), pad4)
    kf = jnp.pad(k.astype(jnp.float32), pad4)
    vf = jnp.pad(v.astype(jnp.float32), pad4)
    cum = jnp.cumsum(jnp.pad(logf.astype(jnp.float32), ((0, 0), (pad, 0), (0, 0))), axis=1)
    cum_k = cum.transpose(0, 2, 1)
    kpos = jnp.arange(Tp)
    key_ok = kpos >= pad
    qb = qf.reshape(B, nb, Q_BLOCK, H, Dh).transpose(1, 0, 2, 3, 4)
    cb = cum_k.reshape(B, H, nb, Q_BLOCK).transpose(2, 0, 1, 3)

    def block(args):
        i, qi, ci = args
        s = jnp.einsum('bqhd,bkhd->bhqk', qi, kf) * scale
        s = s + ci[..., :, None] - cum_k[:, :, None, :]
        qpos = i * Q_BLOCK + jnp.arange(Q_BLOCK)
        mask = (kpos[None, :] <= qpos[:, None]) & key_ok[None, :]
        s = jnp.where(mask[None, None], s, MASK_VALUE)
        p = jax.nn.softmax(s, axis=-1)
        return jnp.einsum('bhqk,bkhd->bqhd', p, vf)

    o = lax.map(block, (jnp.arange(nb), qb, cb))
    o = o.transpose(1, 0, 2, 3, 4).reshape(B, Tp, H, Dh)[:, pad:]
    return o.astype(q.dtype)


def _fox_sample(q, k, v, logf, cache_k, cache_v, cache_logf, page_table, layer):
    S = q.shape[1]
    past = page_table.shape[1] * cache_k.shape[2]
    L = past + S
    scale = HEAD_DIM ** -0.5
    qpos = past + jnp.arange(S)
    mask = jnp.arange(L)[None, :] <= qpos[:, None]

    def one_seq(args):
        qi, ki, vi, lfi, pages = args
        kp = cache_k[layer, pages].reshape(past, ATTN_HEADS, HEAD_DIM)
        vp = cache_v[layer, pages].reshape(past, ATTN_HEADS, HEAD_DIM)
        lp = cache_logf[layer, pages].reshape(past, ATTN_HEADS)
        k_all = jnp.concatenate([kp.astype(jnp.float32), ki.astype(jnp.float32)], axis=0)
        v_all = jnp.concatenate([vp.astype(jnp.float32), vi.astype(jnp.float32)], axis=0)
        lf_all = jnp.concatenate([lp.astype(jnp.float32), lfi.astype(jnp.float32)], axis=0)
        cum = jnp.cumsum(lf_all, axis=0).T
        s = jnp.einsum('qhd,khd->hqk', qi.astype(jnp.float32), k_all) * scale
        s = s + cum[:, past:, None] - cum[:, None, :]
        s = jnp.where(mask[None], s, MASK_VALUE)
        p = jax.nn.softmax(s, axis=-1)
        return jnp.einsum('hqk,khd->qhd', p, v_all)

    o = lax.map(one_seq, (q, k, v, logf, page_table))
    return o.astype(q.dtype)


def _s5_discretize(lam_re, lam_im, log_dt, b_re, b_im):
    dt = jnp.exp(log_dt.astype(jnp.float32))[:, None]
    lr = lam_re.astype(jnp.float32)
    li = lam_im.astype(jnp.float32)
    mag = jnp.exp(lr * dt)
    ab_re = mag * jnp.cos(li * dt)
    ab_im = mag * jnp.sin(li * dt)
    nr = ab_re - 1.0
    ni = ab_im
    den = lr * lr + li * li
    coef_re = (nr * lr + ni * li) / den
    coef_im = (ni * lr - nr * li) / den
    br = b_re.astype(jnp.float32)
    bi = b_im.astype(jnp.float32)
    bb_re = coef_re[..., None] * br - coef_im[..., None] * bi
    bb_im = coef_re[..., None] * bi + coef_im[..., None] * br
    return ab_re, ab_im, bb_re, bb_im


def _cmul_combine(e1, e2):
    a1r, a1i, b1r, b1i = e1
    a2r, a2i, b2r, b2i = e2
    ar = a2r * a1r - a2i * a1i
    ai = a2r * a1i + a2i * a1r
    br = a2r * b1r - a2i * b1i + b2r
    bi = a2r * b1i + a2i * b1r + b2i
    return ar, ai, br, bi


def _s5_scan(u, h0_re, h0_im, ab_re, ab_im, bb_re, bb_im):
    bu_re = jnp.einsum('btgh,gph->btgp', u, bb_re)
    bu_im = jnp.einsum('btgh,gph->btgp', u, bb_im)
    h0r = h0_re.astype(jnp.float32)
    h0i = h0_im.astype(jnp.float32)
    bu_re = bu_re.at[:, 0].add(ab_re * h0r - ab_im * h0i)
    bu_im = bu_im.at[:, 0].add(ab_re * h0i + ab_im * h0r)
    a_re = jnp.broadcast_to(ab_re, bu_re.shape)
    a_im = jnp.broadcast_to(ab_im, bu_im.shape)
    _, _, h_re, h_im = lax.associative_scan(_cmul_combine, (a_re, a_im, bu_re, bu_im), axis=1)
    return h_re, h_im


def _hybrid_layer(x, p, attend, h0_re, h0_im):
    B, T, _ = x.shape
    xn = _rmsnorm(x, p['norm_pre_mix'])
    z = jnp.einsum('btd,dc->btc', xn, p['w_in'])
    q, k, v, fl, u, gl = _split_in(z)
    q = q.reshape(B, T, ATTN_HEADS, HEAD_DIM)
    k = k.reshape(B, T, ATTN_HEADS, HEAD_DIM)
    v = v.reshape(B, T, ATTN_HEADS, HEAD_DIM)
    logf = jax.nn.log_sigmoid(fl.astype(jnp.float32) + p['b_forget'].astype(jnp.float32))
    o_attn = attend(q, k, v, logf).reshape(B, T, ATTN_WIDTH)
    ug = u.astype(jnp.float32).reshape(B, T, SSM_GROUPS, SSM_GROUP)
    ab_re, ab_im, bb_re, bb_im = _s5_discretize(p['ssm_lambda_re'], p['ssm_lambda_im'],
                                                p['ssm_log_dt'], p['ssm_b_re'], p['ssm_b_im'])
    h_re, h_im = _s5_scan(ug, h0_re, h0_im, ab_re, ab_im, bb_re, bb_im)
    y = (jnp.einsum('btgp,ghp->btgh', h_re, p['ssm_c_re'].astype(jnp.float32))
         - jnp.einsum('btgp,ghp->btgh', h_im, p['ssm_c_im'].astype(jnp.float32))
         + p['ssm_d'].astype(jnp.float32).reshape(SSM_GROUPS, SSM_GROUP) * ug)
    y = jax.nn.gelu(y).reshape(B, T, SSM_WIDTH)
    y = (y * jax.nn.sigmoid(jnp.einsum('btc,ce->bte', y, p['w_glu'].astype(jnp.float32)))).astype(x.dtype)
    br_a = jnp.einsum('btc,cd->btd', o_attn, p['w_attn_branch'])
    br_s = jnp.einsum('btc,cd->btd', y, p['w_ssm_branch'])
    g_a, g_s = jnp.split(jax.nn.sigmoid(gl.astype(jnp.float32)), N_BRANCH, axis=-1)
    merged = (g_a * br_a + g_s * br_s).astype(x.dtype)
    mix = jnp.einsum('btd,de->bte', merged, p['w_out'])
    x = x + _rmsnorm(mix, p['norm_post_mix'])
    hn = _rmsnorm(x, p['norm_pre_ffn'])
    hid = jnp.square(jax.nn.relu(jnp.einsum('btd,df->btf', hn, p['w_ffn_up'])))
    ff = jnp.einsum('btf,fd->btd', hid, p['w_ffn_down'])
    x = x + _rmsnorm(ff, p['norm_post_ffn'])
    return x, k, v, logf, h_re[:, -1], h_im[:, -1]


def setup_inputs(seed: int = 0) -> dict:
    key = jax.random.key(seed)
    ks = jax.random.split(key, 32)
    n_pages = PAST_LEN // PAGE_SIZE
    n_used = DEC_BATCH * n_pages
    n_phys = (n_used * 5) // 4
    f32 = jnp.float32

    def nrm(k, shape, scale=1.0):
        return jax.random.normal(k, shape, f32) * scale

    x_prompt = nrm(ks[0], (BATCH, SEQ, D_MODEL))
    x_sample = nrm(ks[1], (DEC_BATCH, DEC_SEQ, D_MODEL))
    cache_k = nrm(ks[2], (DEPTH, n_phys, PAGE_SIZE, ATTN_HEADS, HEAD_DIM))
    cache_v = nrm(ks[3], (DEPTH, n_phys, PAGE_SIZE, ATTN_HEADS, HEAD_DIM))
    cache_logf = jax.nn.log_sigmoid(nrm(ks[4], (DEPTH, n_phys, PAGE_SIZE, ATTN_HEADS)) + 3.0)
    state_ssm_re = nrm(ks[5], (DEPTH, DEC_BATCH, SSM_GROUPS, SSM_STATE), 0.1)
    state_ssm_im = nrm(ks[6], (DEPTH, DEC_BATCH, SSM_GROUPS, SSM_STATE), 0.1)
    page_table = jax.random.permutation(ks[7], n_phys)[:n_used].reshape(DEC_BATCH, n_pages).astype(jnp.int32)

    meta_tokens = nrm(ks[8], (N_META, D_MODEL))
    norm_pre_mix = 1.0 + nrm(ks[9], (DEPTH, D_MODEL), 0.02)
    w_in = nrm(ks[10], (DEPTH, D_MODEL, IN_COLS), D_MODEL ** -0.5)
    b_forget = jax.random.uniform(ks[11], (DEPTH, ATTN_HEADS), f32, 1.0, 5.0)
    n_idx = jnp.arange(SSM_STATE, dtype=f32)
    ssm_lambda_re = -0.5 + nrm(ks[12], (DEPTH, SSM_GROUPS, SSM_STATE), 0.01)
    ssm_lambda_im = math.pi * n_idx + nrm(ks[13], (DEPTH, SSM_GROUPS, SSM_STATE), 0.01)
    ssm_log_dt = jax.random.uniform(ks[14], (DEPTH, SSM_GROUPS), f32, math.log(DT_MIN), math.log(DT_MAX))
    ssm_b_re = nrm(ks[15], (DEPTH, SSM_GROUPS, SSM_STATE, SSM_GROUP), (2 * SSM_GROUP) ** -0.5)
    ssm_b_im = nrm(ks[16], (DEPTH, SSM_GROUPS, SSM_STATE, SSM_GROUP), (2 * SSM_GROUP) ** -0.5)
    ssm_c_re = nrm(ks[17], (DEPTH, SSM_GROUPS, SSM_GROUP, SSM_STATE), (2 * SSM_STATE) ** -0.5)
    ssm_c_im = nrm(ks[18], (DEPTH, SSM_GROUPS, SSM_GROUP, SSM_STATE), (2 * SSM_STATE) ** -0.5)
    ssm_d = nrm(ks[19], (DEPTH, SSM_WIDTH))
    w_glu = nrm(ks[20], (DEPTH, SSM_WIDTH, SSM_WIDTH), SSM_WIDTH ** -0.5)
    w_attn_branch = nrm(ks[21], (DEPTH, ATTN_WIDTH, D_MODEL), ATTN_WIDTH ** -0.5)
    w_ssm_branch = nrm(ks[22], (DEPTH, SSM_WIDTH, D_MODEL), SSM_WIDTH ** -0.5)
    w_out = nrm(ks[23], (DEPTH, D_MODEL, D_MODEL), D_MODEL ** -0.5)
    norm_post_mix = 1.0 + nrm(ks[24], (DEPTH, D_MODEL), 0.02)
    norm_pre_ffn = 1.0 + nrm(ks[25], (DEPTH, D_MODEL), 0.02)
    w_ffn_up = nrm(ks[26], (DEPTH, D_MODEL, D_FF), D_MODEL ** -0.5)
    w_ffn_down = nrm(ks[27], (DEPTH, D_FF, D_MODEL), D_FF ** -0.5)
    norm_post_ffn = 1.0 + nrm(ks[28], (DEPTH, D_MODEL), 0.02)
    return {
        "x_prompt": x_prompt, "x_sample": x_sample,
        "cache_k": cache_k, "cache_v": cache_v, "cache_logf": cache_logf,
        "state_ssm_re": state_ssm_re, "state_ssm_im": state_ssm_im,
        "page_table": page_table,
        "meta_tokens": meta_tokens, "norm_pre_mix": norm_pre_mix, "w_in": w_in,
        "b_forget": b_forget, "ssm_lambda_re": ssm_lambda_re, "ssm_lambda_im": ssm_lambda_im,
        "ssm_log_dt": ssm_log_dt, "ssm_b_re": ssm_b_re, "ssm_b_im": ssm_b_im,
        "ssm_c_re": ssm_c_re, "ssm_c_im": ssm_c_im, "ssm_d": ssm_d, "w_glu": w_glu,
        "w_attn_branch": w_attn_branch, "w_ssm_branch": w_ssm_branch, "w_out": w_out,
        "norm_post_mix": norm_post_mix, "norm_pre_ffn": norm_pre_ffn,
        "w_ffn_up": w_ffn_up, "w_ffn_down": w_ffn_down, "norm_post_ffn": norm_post_ffn,
    }


def reference(x_prompt, x_sample, cache_k, cache_v, cache_logf, state_ssm_re, state_ssm_im,
              page_table, meta_tokens, norm_pre_mix, w_in, b_forget, ssm_lambda_re, ssm_lambda_im,
              ssm_log_dt, ssm_b_re, ssm_b_im, ssm_c_re, ssm_c_im, ssm_d, w_glu, w_attn_branch,
              w_ssm_branch, w_out, norm_post_mix, norm_pre_ffn, w_ffn_up, w_ffn_down, norm_post_ffn):
    B = x_prompt.shape[0]
    meta = jnp.broadcast_to(meta_tokens.astype(x_prompt.dtype)[None], (B, N_META, D_MODEL))
    hp = jnp.concatenate([meta, x_prompt], axis=1)
    hs = x_sample
    zeros_state = jnp.zeros((B, SSM_GROUPS, SSM_STATE), jnp.float32)
    kp_l, vp_l, lp_l, rp_l, ip_l = [], [], [], [], []
    ks_l, vs_l, ls_l, rs_l, is_l = [], [], [], [], []
    for l in range(DEPTH):
        p = {
            'norm_pre_mix': norm_pre_mix[l], 'w_in': w_in[l], 'b_forget': b_forget[l],
            'ssm_lambda_re': ssm_lambda_re[l], 'ssm_lambda_im': ssm_lambda_im[l],
            'ssm_log_dt': ssm_log_dt[l], 'ssm_b_re': ssm_b_re[l], 'ssm_b_im': ssm_b_im[l],
            'ssm_c_re': ssm_c_re[l], 'ssm_c_im': ssm_c_im[l], 'ssm_d': ssm_d[l], 'w_glu': w_glu[l],
            'w_attn_branch': w_attn_branch[l], 'w_ssm_branch': w_ssm_branch[l], 'w_out': w_out[l],
            'norm_post_mix': norm_post_mix[l], 'norm_pre_ffn': norm_pre_ffn[l],
            'w_ffn_up': w_ffn_up[l], 'w_ffn_down': w_ffn_down[l], 'norm_post_ffn': norm_post_ffn[l],
        }
        hp, kp, vp, lp, rp, ip = _hybrid_layer(hp, p, _fox_prompt, zeros_state, zeros_state)
        attend_s = functools.partial(_fox_sample, cache_k=cache_k, cache_v=cache_v,
                                     cache_logf=cache_logf, page_table=page_table, layer=l)
        hs, k_s, v_s, l_s, r_s, i_s = _hybrid_layer(hs, p, attend_s, state_ssm_re[l], state_ssm_im[l])
        kp_l.append(kp); vp_l.append(vp); lp_l.append(lp); rp_l.append(rp); ip_l.append(ip)
        ks_l.append(k_s); vs_l.append(v_s); ls_l.append(l_s); rs_l.append(r_s); is_l.append(i_s)
    y_prompt = hp[:, N_META:]
    y_sample = hs
    return (y_prompt, y_sample,
            jnp.stack(kp_l), jnp.stack(vp_l), jnp.stack(lp_l), jnp.stack(rp_l), jnp.stack(ip_l),
            jnp.stack(ks_l), jnp.stack(vs_l), jnp.stack(ls_l), jnp.stack(rs_l), jnp.stack(is_l))
```

```python
import functools
import math

import numpy as np
import jax
import jax.numpy as jnp
from jax import lax
from jax.experimental import pallas as pl
from jax.experimental.pallas import tpu as pltpu

F32 = jnp.float32
BF16 = jnp.bfloat16

N_META = 16
ATTN_HEADS = 8
HEAD_DIM = 64
ATTN_WIDTH = ATTN_HEADS * HEAD_DIM
SSM_GROUP = 16
SSM_STATE = 64
RMS_EPS = 1e-6
MASK_VALUE = -1e30
QK_SCALE = HEAD_DIM ** -0.5
LANES = 128
SSM_CHUNK = 16
FL_PAD = LANES
VMEM_LIMIT = 56 * 1024 * 1024


def _dot(a, b):
    return jnp.dot(a, b, preferred_element_type=F32)


def _dot_nt(a, b):
    return lax.dot_general(a, b, (((1,), (1,)), ((), ())), preferred_element_type=F32)


def _rms(x, gain):
    inv = lax.rsqrt(jnp.mean(x * x, axis=-1, keepdims=True) + RMS_EPS)
    return x * inv * gain


def _sigmoid(x):
    return 1.0 / (1.0 + jnp.exp(-x))


def _split3(x):
    hi = x.astype(BF16)
    r1 = x - hi.astype(F32)
    mid = r1.astype(BF16)
    lo = (r1 - mid.astype(F32)).astype(BF16)
    return hi, mid, lo


def _const_spec(shape):
    return pl.BlockSpec(shape, lambda *_: (0,) * len(shape))


def _in_proj_kernel(c0_ref, x_ref, g_ref, w_ref, bf_ref,
                    q_ref, k_ref, v_ref, u_ref, gl_ref, lf_ref, nf_ref,
                    carry_ref, *, tiles_per_seq, tm, d_model, n_gate):
    i = pl.program_id(0)

    @pl.when(i % tiles_per_seq == 0)
    def _():
        carry_ref[...] = c0_ref[...]

    xn = _rms(x_ref[...], g_ref[...]).astype(BF16)

    def proj(lo, hi):
        return _dot(xn, w_ref[:, lo:hi])

    aw = ATTN_WIDTH
    sw = d_model // 2
    q_ref[...] = (proj(0, aw) * QK_SCALE).astype(q_ref.dtype)
    k_ref[...] = proj(aw, 2 * aw)
    v_ref[...] = proj(2 * aw, 3 * aw)
    u_ref[...] = proj(3 * aw, 3 * aw + sw).astype(u_ref.dtype)
    g0 = 3 * aw + sw
    gl_ref[...] = proj(g0, g0 + n_gate).astype(gl_ref.dtype)
    fl = proj(g0 + n_gate, g0 + n_gate + FL_PAD) + bf_ref[...]
    lf = jnp.minimum(fl, 0.0) - jnp.log1p(jnp.exp(-jnp.abs(fl)))
    lane = lax.broadcasted_iota(jnp.int32, lf.shape, 1)
    lf = jnp.where(lane < ATTN_HEADS, lf, 0.0)
    lf_ref[...] = lf
    row = lax.broadcasted_iota(jnp.int32, lf.shape, 0)
    cs = lf
    shift = 1
    while shift < tm:
        cs = cs + jnp.where(row >= shift, pltpu.roll(cs, shift, axis=0), 0.0)
        shift *= 2
    f = cs + carry_ref[...]
    carry_ref[...] = f[tm - 1:tm, :]
    hi, mid, lo = _split3(-f)
    nf_ref[:, 0:LANES] = hi
    nf_ref[:, LANES:2 * LANES] = mid
    nf_ref[:, 2 * LANES:3 * LANES] = lo


def _in_proj(x, gain, w1, bf_pad, carry0, *, tm, tiles_per_seq):
    n, d_model = x.shape
    sw = d_model // 2
    n_gate = 2 * d_model
    assert n % tm == 0
    kern = functools.partial(_in_proj_kernel, tiles_per_seq=tiles_per_seq, tm=tm,
                             d_model=d_model, n_gate=n_gate)
    row = lambda w: pl.BlockSpec((tm, w), lambda i: (i, 0))
    out_shape = (
        jax.ShapeDtypeStruct((n, ATTN_WIDTH), BF16),
        jax.ShapeDtypeStruct((n, ATTN_WIDTH), F32),
        jax.ShapeDtypeStruct((n, ATTN_WIDTH), F32),
        jax.ShapeDtypeStruct((n, sw), BF16),
        jax.ShapeDtypeStruct((n, n_gate), BF16),
        jax.ShapeDtypeStruct((n, LANES), F32),
        jax.ShapeDtypeStruct((n, 3 * LANES), BF16),
    )
    return pl.pallas_call(
        kern,
        grid=(n // tm,),
        in_specs=[_const_spec((1, LANES)), row(d_model), _const_spec((1, d_model)),
                  _const_spec(w1.shape), _const_spec((1, FL_PAD))],
        out_specs=(row(ATTN_WIDTH), row(ATTN_WIDTH), row(ATTN_WIDTH), row(sw),
                   row(n_gate), row(LANES), row(3 * LANES)),
        out_shape=out_shape,
        scratch_shapes=[pltpu.VMEM((1, LANES), F32)],
        compiler_params=pltpu.CompilerParams(dimension_semantics=("arbitrary",),
                                             vmem_limit_bytes=VMEM_LIMIT),
        name="in_proj",
    )(carry0, x, gain, w1, bf_pad)


def _ssm_param_kernel(lr_ref, li_ref, ldt_ref, btr_ref, bti_ref, cr_ref, ci_ref,
                      kt_ref, wsr_ref, wsi_ref, wcr_ref, wci_ref, apr_ref, api_ref):
    lr = lr_ref[...]
    li = li_ref[...]
    dt = jnp.exp(ldt_ref[...])
    mag = jnp.exp(lr * dt)
    ar = mag * jnp.cos(li * dt)
    ai = mag * jnp.sin(li * dt)
    nr = ar - 1.0
    ni = ai
    den = lr * lr + li * li
    coef_re = (nr * lr + ni * li) / den
    coef_im = (ni * lr - nr * li) / den
    btr = btr_ref[...]
    bti = bti_ref[...]
    bbr = coef_re * btr - coef_im * bti
    bbi = coef_re * bti + coef_im * btr
    cr = cr_ref[...]
    ci = ci_ref[...]

    def bdot(a, b):
        return lax.dot_general(a, b, (((2,), (2,)), ((0,), (0,))),
                               precision=lax.Precision.HIGHEST, preferred_element_type=F32)

    pr = jnp.ones_like(lr)
    pi = jnp.zeros_like(lr)
    for tau in range(SSM_CHUNK + 1):
        apr_ref[tau] = pr
        api_ref[tau] = pi
        if tau < SSM_CHUNK:
            pbr = pr * bbr - pi * bbi
            pbi = pr * bbi + pi * bbr
            wsr_ref[SSM_CHUNK - 1 - tau] = pbr
            wsi_ref[SSM_CHUNK - 1 - tau] = pbi
            kt_ref[tau] = bdot(cr, pbr) - bdot(ci, pbi)
        if tau >= 1:
            wcr_ref[tau - 1] = cr * pr - ci * pi
            wci_ref[tau - 1] = -(cr * pi + ci * pr)
        pr, pi = pr * ar - pi * ai, pr * ai + pi * ar


def _ssm_params(lam_re, lam_im, log_dt, b_re, b_im, c_re, c_im):
    g, p = lam_re.shape
    h = SSM_GROUP
    L = SSM_CHUNK
    args = (lam_re.reshape(g, 1, p), lam_im.reshape(g, 1, p),
            jnp.broadcast_to(log_dt.reshape(g, 1, 1), (g, 1, p)),
            b_re.transpose(0, 2, 1), b_im.transpose(0, 2, 1), c_re, c_im)
    out_shape = (
        jax.ShapeDtypeStruct((L, g, h, h), F32),
        jax.ShapeDtypeStruct((L, g, h, p), F32),
        jax.ShapeDtypeStruct((L, g, h, p), F32),
        jax.ShapeDtypeStruct((L, g, h, p), F32),
        jax.ShapeDtypeStruct((L, g, h, p), F32),
        jax.ShapeDtypeStruct((L + 1, g, 1, p), F32),
        jax.ShapeDtypeStruct((L + 1, g, 1, p), F32),
    )
    return pl.pallas_call(_ssm_param_kernel, out_shape=out_shape, name="ssm_params")(*args)


def _pair_blockdiag(x):
    g, a, b = x.shape
    x = x.reshape(g // 2, 2, a, b)
    z = jnp.zeros((g // 2, a, b), x.dtype)
    top = jnp.concatenate([x[:, 0], z], axis=2)
    bot = jnp.concatenate([z, x[:, 1]], axis=2)
    return jnp.concatenate([top, bot], axis=1)


def _ssm_operators(params, ssm_d, L):
    kt, wsr, wsi, wcr, wci, apr, api = params
    g = kt.shape[1]
    h = SSM_GROUP
    p = SSM_STATE
    s = np.arange(L)[:, None]
    t = np.arange(L)[None, :]
    idx = np.clip(t - s, 0, SSM_CHUNK - 1)
    mask = jnp.asarray((t >= s), F32)
    ktg = kt.transpose(1, 0, 2, 3)
    t5 = ktg[:, idx] * mask[None, :, :, None, None]
    toep = t5.transpose(0, 1, 4, 2, 3).reshape(g, L * h, L * h)

    def state_w(w):
        return w[SSM_CHUNK - L:].transpose(1, 0, 2, 3).reshape(g, L * h, p)

    def carry_w(w):
        return w[:L].transpose(1, 3, 0, 2).reshape(g, p, L * h)

    d = jnp.broadcast_to(ssm_d.reshape(g, 1, h), (g, L, h)).reshape(g // 2, 1, 2 * L * h)
    return dict(
        toep=_pair_blockdiag(toep).astype(BF16),
        wsr=_pair_blockdiag(state_w(wsr)).astype(BF16),
        wsi=_pair_blockdiag(state_w(wsi)).astype(BF16),
        wcr=_pair_blockdiag(carry_w(wcr)).astype(BF16),
        wci=_pair_blockdiag(carry_w(wci)).astype(BF16),
        d=d.astype(F32),
        ar=apr[L].reshape(g // 2, 1, 2 * p),
        ai=api[L].reshape(g // 2, 1, 2 * p),
    )


def _gelu_tanh(x):
    c = math.sqrt(2.0 / math.pi)
    return x * (0.5 * (1.0 + jnp.tanh(c * (x + 0.044715 * (x * x * x)))))


def _ssm_kernel(u_ref, h0r_ref, h0i_ref, toep_ref, wsr_ref, wsi_ref, wcr_ref, wci_ref,
                d_ref, ar_ref, ai_ref, y_ref, hfr_ref, hfi_ref, *scratch, n_chunks, bk):
    ub = u_ref[...]
    hlr = _dot(ub, wsr_ref[...])
    hli = _dot(ub, wsi_ref[...])
    ar = ar_ref[...]
    ai = ai_ref[...]
    h0r = h0r_ref[...]
    h0i = h0i_ref[...]
    if n_chunks == 1:
        hir, hii = h0r, h0i
        hfr = ar * h0r - ai * h0i + hlr
        hfi = ar * h0i + ai * h0r + hli
    else:
        hlr_sc, hli_sc, hir_sc, hii_sc = scratch
        hlr_sc[...] = hlr
        hli_sc[...] = hli

        def body(c, carry):
            hr, hi = carry
            rows = pl.ds(pl.multiple_of(c * bk, bk), bk)
            hir_sc[rows, :] = hr
            hii_sc[rows, :] = hi
            return (ar * hr - ai * hi + hlr_sc[rows, :], ar * hi + ai * hr + hli_sc[rows, :])

        hfr, hfi = lax.fori_loop(0, n_chunks, body, (h0r, h0i), unroll=8)
        hir = hir_sc[...]
        hii = hii_sc[...]
    y = (_dot(ub, toep_ref[...]) + _dot(hir.astype(BF16), wcr_ref[...])
         + _dot(hii.astype(BF16), wci_ref[...]) + d_ref[...] * ub.astype(F32))
    y_ref[...] = _gelu_tanh(y).astype(y_ref.dtype)
    hfr_ref[...] = hfr
    hfi_ref[...] = hfi


def _ssm_chunk(u_flat, h0r, h0i, ops, *, n_chunks, bk):
    pairs, rows, lw = u_flat.shape
    sl = 2 * SSM_STATE
    assert rows == n_chunks * bk
    kern = functools.partial(_ssm_kernel, n_chunks=n_chunks, bk=bk)
    per_pair = lambda a, b: pl.BlockSpec((None, a, b), lambda i: (i, 0, 0))
    state = pl.BlockSpec((bk, sl), lambda i: (0, i))
    scratch = [] if n_chunks == 1 else [pltpu.VMEM((rows, sl), F32)] * 4
    return pl.pallas_call(
        kern,
        grid=(pairs,),
        in_specs=[per_pair(rows, lw), state, state, per_pair(lw, lw), per_pair(lw, sl),
                  per_pair(lw, sl), per_pair(sl, lw), per_pair(sl, lw), per_pair(1, lw),
                  per_pair(1, sl), per_pair(1, sl)],
        out_specs=(per_pair(rows, lw), state, state),
        out_shape=(jax.ShapeDtypeStruct((pairs, rows, lw), BF16),
                   jax.ShapeDtypeStruct((bk, pairs * sl), F32),
                   jax.ShapeDtypeStruct((bk, pairs * sl), F32)),
        scratch_shapes=scratch,
        compiler_params=pltpu.CompilerParams(dimension_semantics=("arbitrary",),
                                             vmem_limit_bytes=VMEM_LIMIT),
        name="ssm_chunk",
    )(u_flat, h0r, h0i, ops["toep"], ops["wsr"], ops["wsi"], ops["wcr"], ops["wci"],
      ops["d"], ops["ar"], ops["ai"])


def _to_chunks(u, nb, nc, L):
    pairs = u.shape[1] // (2 * SSM_GROUP)
    u = u.reshape(nb, nc, L, pairs, 2, SSM_GROUP).transpose(3, 1, 0, 4, 2, 5)
    return u.reshape(pairs, nc * nb, 2 * L * SSM_GROUP)


def _from_chunks(y, nb, nc, L):
    pairs = y.shape[0]
    y = y.reshape(pairs, nc, nb, 2, L, SSM_GROUP).transpose(2, 1, 4, 0, 3, 5)
    return y.reshape(nb * nc * L, pairs * 2 * SSM_GROUP)


def _attn_prompt_kernel(q_ref, k_ref, v_ref, nf_ref, km_ref, vm_ref, nfm_ref, sel_ref, o_ref,
                        kaug_sc, va_sc, vb_sc, kmaug_sc, vma_sc, vmb_sc, *, tq, n_meta):
    t_len = q_ref.shape[0]
    lane = lax.broadcasted_iota(jnp.int32, (1, LANES), 1)
    is_a = lane < HEAD_DIM

    def stage(k, v, nf, kaug, va, vb, rows):
        kaug[rows, 0:LANES] = k.astype(BF16)
        kaug[rows, LANES:2 * LANES] = _dot(nf, sel_ref[...]).astype(BF16)
        va[rows, :] = jnp.where(is_a, v, 1.0).astype(BF16)
        vb[rows, :] = jnp.where(is_a, 1.0, v).astype(BF16)

    def stage_body(i, _):
        rows = pl.ds(pl.multiple_of(i * tq, tq), tq)
        stage(k_ref[rows, :], v_ref[rows, :], nf_ref[rows, :], kaug_sc, va_sc, vb_sc, rows)
        return 0

    lax.fori_loop(0, t_len // tq, stage_body, 0)
    stage(km_ref[...], vm_ref[...], nfm_ref[...], kmaug_sc, vma_sc, vmb_sc, pl.ds(0, n_meta))

    row_i = lax.broadcasted_iota(jnp.int32, (tq, tq), 0)
    col_i = lax.broadcasted_iota(jnp.int32, (tq, tq), 1)
    causal = col_i <= row_i

    def q_block(i, _):
        r0 = pl.multiple_of(i * tq, tq)
        q = q_ref[pl.ds(r0, tq), :]
        outs = []
        for head in range(2):
            mine = is_a if head == 0 else jnp.logical_not(is_a)
            v_sc, vm_sc = (va_sc, vma_sc) if head == 0 else (vb_sc, vmb_sc)
            ones_at = (lane >= 3 * head) & (lane < 3 * head + 3)
            q_ones = jnp.broadcast_to(jnp.where(ones_at, 1.0, 0.0).astype(BF16), (tq, LANES))
            qaug = jnp.concatenate([jnp.where(mine, q, jnp.zeros_like(q)), q_ones], axis=1)

            s = _dot_nt(qaug, kmaug_sc[...])
            m = jnp.max(s, axis=-1, keepdims=True)
            acc = _dot(jnp.exp(s - m).astype(BF16), vm_sc[...])

            def update(s, v_blk, m, acc):
                m_new = jnp.maximum(m, jnp.max(s, axis=-1, keepdims=True))
                alpha = jnp.exp(m - m_new)
                p = jnp.exp(s - m_new).astype(BF16)
                return m_new, alpha * acc + _dot(p, v_blk)

            def kv_block(kb, carry):
                rows = pl.ds(pl.multiple_of(kb * tq, tq), tq)
                return update(_dot_nt(qaug, kaug_sc[rows, :]), v_sc[rows, :], *carry)

            m, acc = lax.fori_loop(0, i, kv_block, (m, acc))
            rows = pl.ds(r0, tq)
            s = jnp.where(causal, _dot_nt(qaug, kaug_sc[rows, :]), MASK_VALUE)
            m, acc = update(s, v_sc[rows, :], m, acc)
            denom = acc[:, HEAD_DIM:HEAD_DIM + 1] if head == 0 else acc[:, 0:1]
            outs.append(acc / denom)
        o_ref[pl.ds(r0, tq), :] = jnp.where(is_a, outs[0], outs[1]).astype(o_ref.dtype)
        return 0

    lax.fori_loop(0, t_len // tq, q_block, 0)


def _forget_select():
    pairs = ATTN_HEADS // 2
    sel = np.zeros((pairs, 3 * LANES, LANES), np.float32)
    for j in range(pairs):
        for head in range(2):
            for piece in range(3):
                sel[j, piece * LANES + 2 * j + head, 3 * head + piece] = 1.0
    return jnp.asarray(sel, BF16)


def _attn_prompt(q, k, v, nf, k_meta, v_meta, nf_meta, *, batch, seq, tq):
    pairs = ATTN_HEADS // 2
    n_meta = k_meta.shape[0]
    assert seq % tq == 0
    kern = functools.partial(_attn_prompt_kernel, tq=tq, n_meta=n_meta)
    tok = lambda w: pl.BlockSpec((seq, w), lambda b, j: (b, j))
    meta = lambda w: pl.BlockSpec((n_meta, w), lambda b, j: (0, j))
    return pl.pallas_call(
        kern,
        grid=(batch, pairs),
        in_specs=[tok(LANES), tok(LANES), tok(LANES),
                  pl.BlockSpec((seq, 3 * LANES), lambda b, j: (b, 0)),
                  meta(LANES), meta(LANES),
                  pl.BlockSpec((n_meta, 3 * LANES), lambda b, j: (0, 0)),
                  pl.BlockSpec((None, 3 * LANES, LANES), lambda b, j: (j, 0, 0))],
        out_specs=tok(LANES),
        out_shape=jax.ShapeDtypeStruct((batch * seq, ATTN_WIDTH), BF16),
        scratch_shapes=[pltpu.VMEM((seq, 2 * LANES), BF16), pltpu.VMEM((seq, LANES), BF16),
                        pltpu.VMEM((seq, LANES), BF16), pltpu.VMEM((n_meta, 2 * LANES), BF16),
                        pltpu.VMEM((n_meta, LANES), BF16), pltpu.VMEM((n_meta, LANES), BF16)],
        compiler_params=pltpu.CompilerParams(dimension_semantics=("arbitrary", "arbitrary"),
                                             vmem_limit_bytes=VMEM_LIMIT),
        name="attn_prompt",
    )(q, k, v, nf, k_meta, v_meta, nf_meta, _forget_select())


def _attn_sample_kernel(pt_ref, qr_ref, kn_ref, vn_ref, lfn_ref, tri_ref, *rest, pages, n_new):
    k_refs = rest[0:pages]
    v_refs = rest[pages:2 * pages]
    lf_refs = rest[2 * pages:3 * pages]
    o_ref = rest[3 * pages]
    m_sc, l_sc, acc_sc, carry_sc = rest[3 * pages + 1:]
    c = pl.program_id(1)
    qr = qr_ref[...]
    n_rows = qr.shape[0]
    n_q = n_rows // ATTN_HEADS

    @pl.when(c == 0)
    def _init():
        s = _dot_nt(qr, kn_ref[...].astype(BF16))
        lfn = lfn_ref[...]
        nk = lfn.shape[1]
        key = lax.broadcasted_iota(jnp.int32, (1, nk), 1)
        cn = jnp.zeros_like(lfn)
        for j in range(n_new):
            cn = cn + jnp.where(key >= j, lfn[:, j:j + 1], 0.0)
        s = (s.reshape(n_q, ATTN_HEADS, nk) - cn[None]).reshape(n_rows, nk)
        qpos = lax.broadcasted_iota(jnp.int32, (n_rows, 1), 0) // ATTN_HEADS
        s = jnp.where(key <= qpos, s, MASK_VALUE)
        m = jnp.max(s, axis=-1, keepdims=True)
        p = jnp.exp(s - m)
        m_sc[...] = m
        l_sc[...] = jnp.sum(p, axis=-1, keepdims=True)
        acc_sc[...] = _dot(p.astype(BF16), vn_ref[...].astype(BF16))
        carry_sc[...] = jnp.zeros_like(carry_sc)

    lf = jnp.concatenate([r[...] for r in lf_refs], axis=0)
    hi, mid, lo = _split3(lf)
    tri = tri_ref[...]
    suffix = _dot(hi, tri) + _dot(mid, tri) + _dot(lo, tri)
    total = jnp.sum(lf, axis=-1, keepdims=True)
    carry = carry_sc[...]
    s_parts = [None] * pages
    for r in reversed(range(pages)):
        rows = slice(r * ATTN_HEADS, (r + 1) * ATTN_HEADS)
        bias = suffix[rows] + carry
        carry = carry + total[rows]
        s = _dot_nt(qr, k_refs[r][...].astype(BF16))
        s_parts[r] = (s.reshape(n_q, ATTN_HEADS, LANES) + bias[None]).reshape(n_rows, LANES)
    carry_sc[...] = carry
    s = jnp.concatenate(s_parts, axis=1)
    m_prev = m_sc[...]
    m_new = jnp.maximum(m_prev, jnp.max(s, axis=-1, keepdims=True))
    alpha = jnp.exp(m_prev - m_new)
    p = jnp.exp(s - m_new)
    l_sc[...] = alpha * l_sc[...] + jnp.sum(p, axis=-1, keepdims=True)
    pb = p.astype(BF16)
    acc = alpha * acc_sc[...]
    for r in range(pages):
        acc = acc + _dot(pb[:, r * LANES:(r + 1) * LANES], v_refs[r][...].astype(BF16))
    acc_sc[...] = acc
    m_sc[...] = m_new

    @pl.when(c == pl.num_programs(1) - 1)
    def _finish():
        o = acc_sc[...] / l_sc[...]
        width = o.shape[1]
        head_of_lane = lax.broadcasted_iota(jnp.int32, (ATTN_HEADS, width), 1) // HEAD_DIM
        head_of_row = lax.broadcasted_iota(jnp.int32, (ATTN_HEADS, width), 0)
        keep = jnp.where(head_of_lane == head_of_row, 1.0, 0.0)
        o_ref[...] = jnp.sum(o.reshape(n_q, ATTN_HEADS, width) * keep[None], axis=1).astype(o_ref.dtype)


def _attn_sample(q, k_new, v_new, lf_new, cache_k, cache_v, cache_lf, page_table, *, pages):
    db, n_new, width = q.shape
    n_pages = page_table.shape[1]
    page = cache_k.shape[1]
    assert page == LANES and n_pages % pages == 0
    steps = n_pages // pages
    pad = 16 - n_new
    head_eye = jnp.repeat(jnp.eye(ATTN_HEADS, dtype=q.dtype), HEAD_DIM, axis=1)
    q_rows = (q[:, :, None, :] * head_eye[None, None]).reshape(db, n_new * ATTN_HEADS, width)
    kn = jnp.pad(k_new, ((0, 0), (0, pad), (0, 0)))
    vn = jnp.pad(v_new, ((0, 0), (0, pad), (0, 0)))
    lfn = jnp.pad(lf_new.transpose(0, 2, 1), ((0, 0), (0, 0), (0, pad)))
    lf_t = cache_lf.transpose(0, 2, 1)
    tri = jnp.asarray(np.tril(np.ones((page, page), np.float32), -1), BF16)

    def page_spec(r, shape):
        def index(s, c, pt):
            return (pt[s, n_pages - pages * (c + 1) + r], 0, 0)
        return pl.BlockSpec((None,) + shape, index)

    seq_spec = lambda a, b: pl.BlockSpec((None, a, b), lambda s, c, pt: (s, 0, 0))
    in_specs = ([seq_spec(n_new * ATTN_HEADS, width), seq_spec(16, width), seq_spec(16, width),
                 seq_spec(ATTN_HEADS, 16), pl.BlockSpec((page, page), lambda s, c, pt: (0, 0))]
                + [page_spec(r, (page, width)) for r in range(pages)]
                + [page_spec(r, (page, width)) for r in range(pages)]
                + [page_spec(r, (ATTN_HEADS, page)) for r in range(pages)])
    n_rows = n_new * ATTN_HEADS
    grid_spec = pltpu.PrefetchScalarGridSpec(
        num_scalar_prefetch=1, grid=(db, steps), in_specs=in_specs,
        out_specs=seq_spec(n_new, width),
        scratch_shapes=[pltpu.VMEM((n_rows, 1), F32), pltpu.VMEM((n_rows, 1), F32),
                        pltpu.VMEM((n_rows, width), F32), pltpu.VMEM((ATTN_HEADS, 1), F32)])
    kern = functools.partial(_attn_sample_kernel, pages=pages, n_new=n_new)
    return pl.pallas_call(
        kern, grid_spec=grid_spec,
        out_shape=jax.ShapeDtypeStruct((db, n_new, width), BF16),
        compiler_params=pltpu.CompilerParams(dimension_semantics=("arbitrary", "arbitrary"),
                                             vmem_limit_bytes=VMEM_LIMIT),
        name="attn_sample",
    )(page_table, q_rows, kn, vn, lfn, tri, *([cache_k] * pages), *([cache_v] * pages),
      *([lf_t] * pages))


def _post_kernel(x_ref, o_ref, y_ref, gl_ref, wglu_ref, wab_ref, wsb_ref, wout_ref, wup_ref,
                 wdn_ref, g1_ref, g2_ref, g3_ref, out_ref, *, ff_chunk):
    d_model = x_ref.shape[1]
    y = y_ref[...]
    yg = (y.astype(F32) * _sigmoid(_dot(y, wglu_ref[...]))).astype(BF16)
    br_a = _dot(o_ref[...], wab_ref[...])
    br_s = _dot(yg, wsb_ref[...])
    g_a = _sigmoid(gl_ref[:, 0:d_model].astype(F32))
    g_s = _sigmoid(gl_ref[:, d_model:2 * d_model].astype(F32))
    merged = (g_a * br_a + g_s * br_s).astype(BF16)
    x1 = x_ref[...] + _rms(_dot(merged, wout_ref[...]), g1_ref[...])
    hn = _rms(x1, g2_ref[...]).astype(BF16)
    d_ff = wup_ref.shape[1]
    ff = jnp.zeros_like(x1)
    for c0 in range(0, d_ff, ff_chunk):
        hid = jnp.maximum(_dot(hn, wup_ref[:, c0:c0 + ff_chunk]), 0.0)
        ff = ff + _dot((hid * hid).astype(BF16), wdn_ref[c0:c0 + ff_chunk, :])
    out_ref[...] = x1 + _rms(ff, g3_ref[...])


def _post(x, o_attn, y, gl, w, *, tm):
    n, d_model = x.shape
    assert n % tm == 0
    row = lambda wd: pl.BlockSpec((tm, wd), lambda i: (i, 0))
    once = lambda a: pl.BlockSpec(a.shape, lambda i: (0,) * a.ndim, pipeline_mode=pl.Buffered(1))
    weights = (w["glu"], w["attn_branch"], w["ssm_branch"], w["out"], w["up"], w["down"],
               w["g_post_mix"], w["g_pre_ffn"], w["g_post_ffn"])
    return pl.pallas_call(
        functools.partial(_post_kernel, ff_chunk=1024),
        grid=(n // tm,),
        in_specs=[row(d_model), row(o_attn.shape[1]), row(y.shape[1]), row(gl.shape[1])]
                 + [once(a) for a in weights],
        out_specs=row(d_model),
        out_shape=jax.ShapeDtypeStruct((n, d_model), F32),
        compiler_params=pltpu.CompilerParams(dimension_semantics=("arbitrary",),
                                             vmem_limit_bytes=VMEM_LIMIT),
        name="post",
    )(x, o_attn, y, gl, *weights)


def kernel(x_prompt, x_sample, cache_k, cache_v, cache_logf, state_ssm_re, state_ssm_im, page_table,
           meta_tokens, norm_pre_mix, w_in, b_forget, ssm_lambda_re, ssm_lambda_im, ssm_log_dt,
           ssm_b_re, ssm_b_im, ssm_c_re, ssm_c_im, ssm_d, w_glu, w_attn_branch, w_ssm_branch, w_out,
           norm_post_mix, norm_pre_ffn, w_ffn_up, w_ffn_down, norm_post_ffn):
    assert w_in.shape[0] == 1, "one trunk layer"
    batch, seq, d_model = x_prompt.shape
    db, n_new, _ = x_sample.shape
    n_meta = meta_tokens.shape[0]
    assert n_meta == N_META == SSM_CHUNK and seq % SSM_CHUNK == 0
    aw = ATTN_WIDTH
    sw = d_model // 2
    n_phys, page = cache_k.shape[1], cache_k.shape[2]

    wi = w_in[0]
    w1 = jnp.concatenate(
        [wi[:, 0:3 * aw], wi[:, 3 * aw + ATTN_HEADS:3 * aw + ATTN_HEADS + sw],
         wi[:, 3 * aw + ATTN_HEADS + sw:], wi[:, 3 * aw:3 * aw + ATTN_HEADS],
         jnp.zeros((d_model, FL_PAD - ATTN_HEADS), wi.dtype)], axis=1).astype(BF16)
    bf_pad = jnp.pad(b_forget[0].astype(F32), (0, FL_PAD - ATTN_HEADS)).reshape(1, FL_PAD)
    g_pre = norm_pre_mix[0].astype(F32).reshape(1, d_model)
    post_w = dict(glu=w_glu[0].astype(BF16), attn_branch=w_attn_branch[0].astype(BF16),
                  ssm_branch=w_ssm_branch[0].astype(BF16), out=w_out[0].astype(BF16),
                  up=w_ffn_up[0].astype(BF16), down=w_ffn_down[0].astype(BF16),
                  g_post_mix=norm_post_mix[0].astype(F32).reshape(1, d_model),
                  g_pre_ffn=norm_pre_ffn[0].astype(F32).reshape(1, d_model),
                  g_post_ffn=norm_post_ffn[0].astype(F32).reshape(1, d_model))

    xs = jnp.concatenate([meta_tokens.astype(F32), x_sample.reshape(db * n_new, d_model)], axis=0)
    zero_carry = jnp.zeros((1, LANES), F32)
    qs, ks, vs, us, gls, lfs, nfs = _in_proj(xs, g_pre, w1, bf_pad, zero_carry,
                                             tm=xs.shape[0], tiles_per_seq=1)
    meta_carry = -(nfs[n_meta - 1:n_meta, 0:LANES].astype(F32)
                   + nfs[n_meta - 1:n_meta, LANES:2 * LANES].astype(F32)
                   + nfs[n_meta - 1:n_meta, 2 * LANES:].astype(F32))
    tm = 512
    qp, kp, vp, up, glp, lfp, nfp = _in_proj(x_prompt.reshape(batch * seq, d_model), g_pre, w1,
                                             bf_pad, meta_carry, tm=tm, tiles_per_seq=seq // tm)

    params = _ssm_params(ssm_lambda_re[0], ssm_lambda_im[0], ssm_log_dt[0], ssm_b_re[0],
                         ssm_b_im[0], ssm_c_re[0], ssm_c_im[0])
    ops_chunk = _ssm_operators(params, ssm_d[0], SSM_CHUNK)
    ops_new = _ssm_operators(params, ssm_d[0], n_new)
    n_state = (sw // SSM_GROUP) * SSM_STATE
    zeros_state = jnp.zeros((8, n_state), F32)
    u_meta = jnp.broadcast_to(_to_chunks(us[:n_meta], 1, 1, SSM_CHUNK), (sw // 32, 8, 2 * SSM_CHUNK * SSM_GROUP))
    _, hm_re, hm_im = _ssm_chunk(u_meta, zeros_state, zeros_state, ops_chunk, n_chunks=1, bk=8)
    assert batch == 8
    nc = seq // SSM_CHUNK
    yp_flat, hp_re, hp_im = _ssm_chunk(_to_chunks(up, batch, nc, SSM_CHUNK), hm_re, hm_im,
                                       ops_chunk, n_chunks=nc, bk=batch)
    yp = _from_chunks(yp_flat, batch, nc, SSM_CHUNK)
    ys_flat, hs_re, hs_im = _ssm_chunk(_to_chunks(us[n_meta:], db, 1, n_new),
                                       state_ssm_re[0].reshape(db, n_state).astype(F32),
                                       state_ssm_im[0].reshape(db, n_state).astype(F32),
                                       ops_new, n_chunks=1, bk=db)
    ys = _from_chunks(ys_flat, db, 1, n_new)

    op = _attn_prompt(qp, kp, vp, nfp, ks[:n_meta], vs[:n_meta], nfs[:n_meta],
                      batch=batch, seq=seq, tq=512)
    osamp = _attn_sample(qs[n_meta:].reshape(db, n_new, aw), ks[n_meta:].reshape(db, n_new, aw),
                         vs[n_meta:].reshape(db, n_new, aw),
                         lfs[n_meta:, :ATTN_HEADS].reshape(db, n_new, ATTN_HEADS),
                         cache_k[0].reshape(n_phys, page, aw), cache_v[0].reshape(n_phys, page, aw),
                         cache_logf[0].astype(F32), page_table, pages=8)

    y_prompt = _post(x_prompt.reshape(batch * seq, d_model), op, yp, glp, post_w, tm=512)
    y_sample = _post(x_sample.reshape(db * n_new, d_model), osamp.reshape(db * n_new, aw), ys,
                     gls[n_meta:], post_w, tm=db * n_new)

    def with_meta(meta_rows, rows, tail):
        meta_b = jnp.broadcast_to(meta_rows[None], (batch,) + meta_rows.shape)
        full = jnp.concatenate([meta_b, rows.reshape((batch, seq) + meta_rows.shape[1:])], axis=1)
        return full.reshape((1, batch, n_meta + seq) + tail)

    heads = (ATTN_HEADS, HEAD_DIM)
    groups = (sw // SSM_GROUP, SSM_STATE)
    return (
        y_prompt.reshape(batch, seq, d_model),
        y_sample.reshape(db, n_new, d_model),
        with_meta(ks[:n_meta], kp, heads),
        with_meta(vs[:n_meta], vp, heads),
        with_meta(lfs[:n_meta, :ATTN_HEADS], lfp[:, :ATTN_HEADS], (ATTN_HEADS,)),
        hp_re.reshape((1, batch) + groups),
        hp_im.reshape((1, batch) + groups),
        ks[n_meta:].reshape((1, db, n_new) + heads),
        vs[n_meta:].reshape((1, db, n_new) + heads),
        lfs[n_meta:, :ATTN_HEADS].reshape(1, db, n_new, ATTN_HEADS),
        hs_re.reshape((1, db) + groups),
        hs_im.reshape((1, db) + groups),
    )
```
